```python
import jax, jax.numpy as jnp
from jax import lax
import numpy as np

D_MODEL = 1024
BATCH = 32
SEQ = 2048
DEPTH = 1
DEC_BATCH = 4
DEC_SEQ = 4096
PAST_LEN = 128

GRID_W = 64
N_ATTN_HEADS = 8
HEAD_DIM = 64
D_ATTN = N_ATTN_HEADS * HEAD_DIM
D_CONV = D_MODEL - D_ATTN
N_CONV_GROUPS = 8
CONV_WIDTH = 3
NA_ROWS = 8
NA_COLS = 16
D_FF = 2816
D_IN = 3 * D_ATTN + 3 * D_CONV
EPS = 1e-6

kernel_name = "hymba_natten_shortconv_macaron_encoder"


def rms_norm(x, g):
    xf = x.astype(jnp.float32)
    y = xf * lax.rsqrt(jnp.mean(xf * xf, axis=-1, keepdims=True) + EPS)
    return (y * g.astype(jnp.float32)).astype(x.dtype)


def swiglu(x, w_gate, w_up, w_down):
    return (jax.nn.silu(x @ w_gate) * (x @ w_up)) @ w_down


def neighborhood_attention(q, k, v, rpb):
    b, s, h, dh = q.shape
    rows = s // GRID_W
    kh = min(NA_ROWS, rows)
    kw = NA_COLS
    qg = q.reshape(b, rows, GRID_W, h, dh)
    kg = k.reshape(b, rows, GRID_W, h, dh)
    vg = v.reshape(b, rows, GRID_W, h, dh)
    cols = jnp.arange(GRID_W)
    col_start = jnp.clip(cols - kw // 2, 0, GRID_W - kw)
    col_idx = col_start[:, None] + jnp.arange(kw)[None, :]
    col_off = col_idx - cols[:, None]
    bias_cols = rpb[:, :, col_off + NA_COLS - 1]
    scale = HEAD_DIM ** -0.5

    def one_row(r):
        rs = jnp.clip(r - kh // 2, 0, rows - kh)
        q_r = lax.dynamic_index_in_dim(qg, r, axis=1, keepdims=False)
        k_blk = lax.dynamic_slice_in_dim(kg, rs, kh, axis=1)
        v_blk = lax.dynamic_slice_in_dim(vg, rs, kh, axis=1)
        k_win = k_blk[:, :, col_idx]
        v_win = v_blk[:, :, col_idx]
        row_off = rs + jnp.arange(kh) - r
        bias = jnp.transpose(bias_cols[:, row_off + NA_ROWS - 1], (0, 2, 1, 3))
        sc = jnp.einsum('bchd,bicjhd->bhcij', q_r, k_win).astype(jnp.float32) * scale
        sc = sc + bias[None].astype(jnp.float32)
        p = jax.nn.softmax(sc.reshape(b, h, GRID_W, kh * kw), axis=-1)
        p = p.reshape(b, h, GRID_W, kh, kw).astype(v.dtype)
        return jnp.einsum('bhcij,bicjhd->bchd', p, v_win)

    out = lax.map(one_row, jnp.arange(rows))
    return jnp.transpose(out, (1, 0, 2, 3, 4)).reshape(b, s, h, dh)


def short_gated_conv(gb, gc, xin, conv_w):
    s = xin.shape[1]
    u = gc * xin
    half = CONV_WIDTH // 2
    up = jnp.pad(u, ((0, 0), (half, CONV_WIDTH - 1 - half), (0, 0)))
    y = up[:, 0:s] * conv_w[0]
    for j in range(1, CONV_WIDTH):
        y = y + up[:, j:j + s] * conv_w[j]
    return gb * y


def encoder_layer(x, ffn1_norm, ffn1_w_gate, ffn1_w_up, ffn1_w_down, mix_norm, w_in,
                  q_norm, k_norm, rel_pos_bias, conv_w, attn_out_norm, conv_out_norm, w_out,
                  ffn2_norm, ffn2_w_gate, ffn2_w_up, ffn2_w_down, final_norm):
    b, s, _ = x.shape
    h = x + 0.5 * swiglu(rms_norm(x, ffn1_norm), ffn1_w_gate, ffn1_w_up, ffn1_w_down)
    u = rms_norm(h, mix_norm)
    z = u @ w_in
    q, k, v, gb, gc, xin = jnp.split(
        z, [D_ATTN, 2 * D_ATTN, 3 * D_ATTN, 3 * D_ATTN + D_CONV, 3 * D_ATTN + 2 * D_CONV], axis=-1)
    q = rms_norm(q.reshape(b, s, N_ATTN_HEADS, HEAD_DIM), q_norm)
    k = rms_norm(k.reshape(b, s, N_ATTN_HEADS, HEAD_DIM), k_norm)
    v = v.reshape(b, s, N_ATTN_HEADS, HEAD_DIM)
    a = neighborhood_attention(q, k, v, rel_pos_bias)
    a = rms_norm(a, attn_out_norm.reshape(N_ATTN_HEADS, HEAD_DIM)).reshape(b, s, D_ATTN)
    c = short_gated_conv(gb, gc, xin, conv_w)
    gdim = D_CONV // N_CONV_GROUPS
    c = rms_norm(c.reshape(b, s, N_CONV_GROUPS, gdim),
                 conv_out_norm.reshape(N_CONV_GROUPS, gdim)).reshape(b, s, D_CONV)
    h = h + jnp.concatenate([a, c], axis=-1) @ w_out
    h = h + 0.5 * swiglu(rms_norm(h, ffn2_norm), ffn2_w_gate, ffn2_w_up, ffn2_w_down)
    return rms_norm(h, final_norm)


def setup_inputs(seed: int = 0) -> dict:
    key = jax.random.key(seed)
    ks = jax.random.split(key, 24)

    def w(k, shape, fan_in):
        return jax.random.normal(k, shape, jnp.float32) * (fan_in ** -0.5)

    def gain(k, shape):
        return 1.0 + 0.02 * jax.random.normal(k, shape, jnp.float32)

    L = DEPTH
    return {
        "x_prompt": jax.random.normal(ks[0], (BATCH, SEQ, D_MODEL), jnp.float32),
        "x_sample": jax.random.normal(ks[1], (DEC_BATCH, DEC_SEQ, D_MODEL), jnp.float32),
        "ffn1_norm": gain(ks[2], (L, D_MODEL)),
        "ffn1_w_gate": w(ks[3], (L, D_MODEL, D_FF), D_MODEL),
        "ffn1_w_up": w(ks[4], (L, D_MODEL, D_FF), D_MODEL),
        "ffn1_w_down": w(ks[5], (L, D_FF, D_MODEL), D_FF),
        "mix_norm": gain(ks[6], (L, D_MODEL)),
        "w_in": w(ks[7], (L, D_MODEL, D_IN), D_MODEL),
        "q_norm": gain(ks[8], (L, HEAD_DIM)),
        "k_norm": gain(ks[9], (L, HEAD_DIM)),
        "rel_pos_bias": 0.1 * jax.random.normal(ks[10], (L, N_ATTN_HEADS, 2 * NA_ROWS - 1, 2 * NA_COLS - 1), jnp.float32),
        "conv_w": w(ks[11], (L, CONV_WIDTH, D_CONV), CONV_WIDTH),
        "attn_out_norm": gain(ks[12], (L, D_ATTN)),
        "conv_out_norm": gain(ks[13], (L, D_CONV)),
        "w_out": w(ks[14], (L, D_MODEL, D_MODEL), D_MODEL),
        "ffn2_norm": gain(ks[15], (L, D_MODEL)),
        "ffn2_w_gate": w(ks[16], (L, D_MODEL, D_FF), D_MODEL),
        "ffn2_w_up": w(ks[17], (L, D_MODEL, D_FF), D_MODEL),
        "ffn2_w_down": w(ks[18], (L, D_FF, D_MODEL), D_FF),
        "final_norm": gain(ks[19], (L, D_MODEL)),
    }


def reference(x_prompt, x_sample, ffn1_norm, ffn1_w_gate, ffn1_w_up, ffn1_w_down, mix_norm, w_in,
              q_norm, k_norm, rel_pos_bias, conv_w, attn_out_norm, conv_out_norm, w_out,
              ffn2_norm, ffn2_w_gate, ffn2_w_up, ffn2_w_down, final_norm):
    y_prompt = x_prompt
    y_sample = x_sample
    for l in range(DEPTH):
        p = (ffn1_norm[l], ffn1_w_gate[l], ffn1_w_up[l], ffn1_w_down[l], mix_norm[l], w_in[l],
             q_norm[l], k_norm[l], rel_pos_bias[l], conv_w[l], attn_out_norm[l], conv_out_norm[l], w_out[l],
             ffn2_norm[l], ffn2_w_gate[l], ffn2_w_up[l], ffn2_w_down[l], final_norm[l])
        y_prompt = encoder_layer(y_prompt, *p)
        y_sample = encoder_layer(y_sample, *p)
    return (y_prompt, y_sample)
```

```python
import functools

import numpy as np
import jax
import jax.numpy as jnp
from jax import lax
from jax.experimental import pallas as pl
from jax.experimental.pallas import tpu as pltpu

D_MODEL = 1024
GRID_W = 64
N_HEADS = 8
HEAD_DIM = 64
D_ATTN = N_HEADS * HEAD_DIM
D_CONV = D_MODEL - D_ATTN
N_CONV_GROUPS = 8
GROUP_DIM = D_CONV // N_CONV_GROUPS
NA_ROWS = 8
NA_COLS = 16
D_FF = 2816
EPS = 1e-6

LANES = 128
BF16_ROWS = 16
VMEM_LIMIT_BYTES = 56 * 1024 * 1024
TOKEN_TILE = 512
ROWS_PER_TILE = 8
FF_CHUNK = 1024
MASK_VALUE = -1e30
Z_COLS = 5 * D_ATTN

_BF16 = jnp.bfloat16
_F32 = jnp.float32


def _const_spec(shape):
    nd = len(shape)
    return pl.BlockSpec(shape, lambda *_: (0,) * nd, pipeline_mode=pl.Buffered(1))


def _rms_rows(x, gain):
    ms = jnp.mean(x * x, axis=-1, keepdims=True)
    return (x * lax.rsqrt(ms + EPS)) * gain


def _group64_norm(x, gain):
    rows, cols = x.shape
    lo = lax.broadcasted_iota(jnp.int32, (rows, LANES), 1) < HEAD_DIM
    outs = []
    for j in range(cols // LANES):
        t = x[:, j * LANES:(j + 1) * LANES]
        sq = t * t
        s_lo = jnp.sum(jnp.where(lo, sq, 0.0), axis=-1, keepdims=True)
        s_hi = jnp.sum(jnp.where(lo, 0.0, sq), axis=-1, keepdims=True)
        r = jnp.where(lo, lax.rsqrt(s_lo * (1.0 / HEAD_DIM) + EPS), lax.rsqrt(s_hi * (1.0 / HEAD_DIM) + EPS))
        outs.append((t * r) * gain[:, j * LANES:(j + 1) * LANES])
    return outs


def _swiglu_residual(x, norm_gain, wg_ref, wu_ref, wd_ref):
    xn = _rms_rows(x, norm_gain).astype(_BF16)
    acc = None
    for c0 in range(0, D_FF, FF_CHUNK):
        c1 = min(c0 + FF_CHUNK, D_FF)
        g = jnp.dot(xn, wg_ref[:, c0:c1], preferred_element_type=_F32)
        u = jnp.dot(xn, wu_ref[:, c0:c1], preferred_element_type=_F32)
        a = ((g / (1.0 + jnp.exp(-g))) * u).astype(_BF16)
        d = jnp.dot(a, wd_ref[c0:c1, :], preferred_element_type=_F32)
        acc = d if acc is None else acc + d
    return x + 0.5 * acc


def _ffn_inproj_kernel(x_ref, g1_ref, wg_ref, wu_ref, wd_ref, g2_ref, win_ref, qg_ref, kg_ref, h_ref, z_ref):
    h = _swiglu_residual(x_ref[...], g1_ref[...], wg_ref, wu_ref, wd_ref)
    h_ref[...] = h
    un = _rms_rows(h, g2_ref[...]).astype(_BF16)

    def proj(j):
        return jnp.dot(un, win_ref[:, j * D_ATTN:(j + 1) * D_ATTN], preferred_element_type=_F32)

    for j, gain_ref in ((0, qg_ref), (1, kg_ref)):
        parts = _group64_norm(proj(j), gain_ref[...])
        for t, part in enumerate(parts):
            z_ref[:, j * D_ATTN + t * LANES: j * D_ATTN + (t + 1) * LANES] = part.astype(_BF16)
    z_ref[:, 2 * D_ATTN:3 * D_ATTN] = proj(2).astype(_BF16)
    z_ref[:, 3 * D_ATTN:4 * D_ATTN] = proj(3).astype(_BF16)
    z_ref[:, 4 * D_ATTN:5 * D_ATTN] = (proj(4) * proj(5)).astype(_BF16)


def _ffn_inproj(x2d, g1, wg, wu, wd, g2, win, qg, kg):
    n = x2d.shape[0]
    tm = TOKEN_TILE
    assert n % tm == 0
    return pl.pallas_call(
        _ffn_inproj_kernel,
        grid=(n // tm,),
        in_specs=[
            pl.BlockSpec((tm, D_MODEL), lambda i: (i, 0)),
            _const_spec((1, D_MODEL)),
            _const_spec((D_MODEL, D_FF)),
            _const_spec((D_MODEL, D_FF)),
            _const_spec((D_FF, D_MODEL)),
            _const_spec((1, D_MODEL)),
            _const_spec((D_MODEL, 6 * D_ATTN)),
            _const_spec((1, D_ATTN)),
            _const_spec((1, D_ATTN)),
        ],
        out_specs=[
            pl.BlockSpec((tm, D_MODEL), lambda i: (i, 0)),
            pl.BlockSpec((tm, Z_COLS), lambda i: (i, 0)),
        ],
        out_shape=[
            jax.ShapeDtypeStruct((n, D_MODEL), _F32),
            jax.ShapeDtypeStruct((n, Z_COLS), _BF16),
        ],
        compiler_params=pltpu.CompilerParams(
            dimension_semantics=("parallel",), vmem_limit_bytes=VMEM_LIMIT_BYTES),
        name="ffn1_inproj",
    )(x2d, g1, wg, wu, wd, g2, win, qg, kg)


def _mixer_kernel(q_ref, k_ref, v_ref, gb_ref, u_ref, h_ref, bias_ref, cw_ref, ag_ref, cg_ref, wo_ref,
                  o_ref, attn_scr, *, rows, seq):
    i = pl.program_id(1)
    n_tiles = pl.num_programs(1)
    tq = ROWS_PER_TILE * GRID_W
    n_keys = NA_ROWS * GRID_W
    head_of_lane = lax.broadcasted_iota(jnp.int32, (GRID_W, D_ATTN), 1) // HEAD_DIM

    for rr in range(ROWS_PER_TILE):
        r = i * ROWS_PER_TILE + rr
        rs = jnp.clip(r - NA_ROWS // 2, 0, rows - NA_ROWS)
        key0 = pl.multiple_of(rs * GRID_W, GRID_W)
        kblk = k_ref[pl.ds(key0, n_keys), :]
        vblk = v_ref[pl.ds(key0, n_keys), :]
        q_row = q_ref[rr * GRID_W:(rr + 1) * GRID_W, :]
        q_bd = jnp.concatenate(
            [jnp.where(head_of_lane == hd, q_row, jnp.zeros_like(q_row)) for hd in range(N_HEADS)], axis=0)
        s = lax.dot_general(q_bd, kblk, (((1,), (1,)), ((), ())), preferred_element_type=_F32)
        s = s + bias_ref[r - rs]
        m = jnp.max(s, axis=-1, keepdims=True)
        p = jnp.exp(s - m)
        l = jnp.sum(p, axis=-1, keepdims=True)
        o = jnp.dot(p.astype(_BF16), vblk, preferred_element_type=_F32) / l
        o_row = None
        for hd in range(N_HEADS):
            part = jnp.where(head_of_lane == hd, o[hd * GRID_W:(hd + 1) * GRID_W, :], 0.0)
            o_row = part if o_row is None else o_row + part
        attn_scr[rr * GRID_W:(rr + 1) * GRID_W, :] = o_row

    a_parts = _group64_norm(attn_scr[...], ag_ref[...])

    t0 = pl.multiple_of(i * tq, tq)
    u_c = u_ref[pl.ds(t0, tq), :].astype(_F32)
    prev_blk = u_ref[pl.ds(pl.multiple_of(jnp.maximum(t0 - BF16_ROWS, 0), BF16_ROWS), BF16_ROWS), :].astype(_F32)
    next_blk = u_ref[pl.ds(pl.multiple_of(jnp.minimum(t0 + tq, seq - BF16_ROWS), BF16_ROWS), BF16_ROWS), :].astype(_F32)
    prev_row = prev_blk[BF16_ROWS - 1:BF16_ROWS, :] * (i > 0).astype(_F32)
    next_row = next_blk[0:1, :] * (i < n_tiles - 1).astype(_F32)
    tok = lax.broadcasted_iota(jnp.int32, (tq, D_CONV), 0)
    u_prev = jnp.where(tok == 0, prev_row, pltpu.roll(u_c, 1, 0))
    u_next = jnp.where(tok == tq - 1, next_row, pltpu.roll(u_c, tq - 1, 0))
    cw = cw_ref[...]
    y = u_prev * cw[0:1, :] + u_c * cw[1:2, :] + u_next * cw[2:3, :]
    c = gb_ref[...].astype(_F32) * y
    c_parts = _group64_norm(c, cg_ref[...])

    mixed = jnp.concatenate([t.astype(_BF16) for t in a_parts + c_parts], axis=-1)
    o_ref[...] = h_ref[...] + jnp.dot(mixed, wo_ref[...], preferred_element_type=_F32)


def _mixer(z3d, h3d, bias_tab, conv_w, attn_gain, conv_gain, w_out):
    b, s, _ = z3d.shape
    rows = s // GRID_W
    tq = ROWS_PER_TILE * GRID_W
    assert s % tq == 0 and rows >= NA_ROWS

    def tile(width, col):
        return pl.BlockSpec((None, tq, width), lambda bi, ti: (bi, ti, col))

    def whole_seq(col):
        return pl.BlockSpec((None, s, D_ATTN), lambda bi, ti: (bi, 0, col))

    return pl.pallas_call(
        functools.partial(_mixer_kernel, rows=rows, seq=s),
        grid=(b, s // tq),
        in_specs=[
            tile(D_ATTN, 0),
            whole_seq(1),
            whole_seq(2),
            tile(D_ATTN, 3),
            whole_seq(4),
            tile(D_MODEL, 0),
            _const_spec(bias_tab.shape),
            _const_spec(conv_w.shape),
            _const_spec((1, D_ATTN)),
            _const_spec((1, D_CONV)),
            _const_spec((D_MODEL, D_MODEL)),
        ],
        out_specs=pl.BlockSpec((None, tq, D_MODEL), lambda bi, ti: (bi, ti, 0)),
        out_shape=jax.ShapeDtypeStruct((b, s, D_MODEL), _F32),
        scratch_shapes=[pltpu.VMEM((tq, D_ATTN), _F32)],
        compiler_params=pltpu.CompilerParams(
            dimension_semantics=("parallel", "parallel"), vmem_limit_bytes=VMEM_LIMIT_BYTES),
        name="mixer",
    )(z3d, z3d, z3d, z3d, z3d, h3d, bias_tab, conv_w, attn_gain, conv_gain, w_out)


def _ffn_out_kernel(x_ref, g1_ref, wg_ref, wu_ref, wd_ref, gf_ref, y_ref):
    h = _swiglu_residual(x_ref[...], g1_ref[...], wg_ref, wu_ref, wd_ref)
    y_ref[...] = _rms_rows(h, gf_ref[...])


def _ffn_out(x2d, g1, wg, wu, wd, gf):
    n = x2d.shape[0]
    tm = TOKEN_TILE
    assert n % tm == 0
    return pl.pallas_call(
        _ffn_out_kernel,
        grid=(n // tm,),
        in_specs=[
            pl.BlockSpec((tm, D_MODEL), lambda i: (i, 0)),
            _const_spec((1, D_MODEL)),
            _const_spec((D_MODEL, D_FF)),
            _const_spec((D_MODEL, D_FF)),
            _const_spec((D_FF, D_MODEL)),
            _const_spec((1, D_MODEL)),
        ],
        out_specs=pl.BlockSpec((tm, D_MODEL), lambda i: (i, 0)),
        out_shape=jax.ShapeDtypeStruct((n, D_MODEL), _F32),
        compiler_params=pltpu.CompilerParams(
            dimension_semantics=("parallel",), vmem_limit_bytes=VMEM_LIMIT_BYTES),
        name="ffn2_final",
    )(x2d, g1, wg, wu, wd, gf)


def _bias_table(rpb):
    dd = np.arange(NA_ROWS)[:, None]
    ki = np.arange(NA_ROWS)[None, :]
    row_idx = ki - dd + NA_ROWS - 1
    qc = np.arange(GRID_W)[:, None]
    kc = np.arange(GRID_W)[None, :]
    start = np.clip(qc - NA_COLS // 2, 0, GRID_W - NA_COLS)
    valid = (kc >= start) & (kc < start + NA_COLS)
    col_idx = np.clip(kc - qc + NA_COLS - 1, 0, 2 * NA_COLS - 2)
    g = rpb[:, row_idx]
    g = g[:, :, :, col_idx]
    g = jnp.where(valid[None, None, None], g, MASK_VALUE)
    g = jnp.transpose(g, (1, 0, 3, 2, 4))
    return g.reshape(NA_ROWS, N_HEADS * GRID_W, NA_ROWS * GRID_W).astype(_F32)


def _layer(x, p):
    b, s, _ = x.shape
    h, z = _ffn_inproj(x.reshape(b * s, D_MODEL), p["g1"], p["wg1"], p["wu1"], p["wd1"], p["gm"], p["win"],
                       p["qg"], p["kg"])
    h2 = _mixer(z.reshape(b, s, Z_COLS), h.reshape(b, s, D_MODEL), p["bias"], p["cw"], p["ag"], p["cg"], p["wo"])
    y = _ffn_out(h2.reshape(b * s, D_MODEL), p["g2"], p["wg2"], p["wu2"], p["wd2"], p["gf"])
    return y.reshape(b, s, D_MODEL)


def kernel(x_prompt, x_sample, ffn1_norm, ffn1_w_gate, ffn1_w_up, ffn1_w_down, mix_norm, w_in, q_norm, k_norm, rel_pos_bias, conv_w, attn_out_norm, conv_out_norm, w_out, ffn2_norm, ffn2_w_gate, ffn2_w_up, ffn2_w_down, final_norm):
    depth = ffn1_norm.shape[0]
    y_prompt, y_sample = x_prompt, x_sample
    for l in range(depth):
        row = lambda a: a[l].reshape(1, -1).astype(_F32)
        p = dict(
            g1=row(ffn1_norm), wg1=ffn1_w_gate[l].astype(_BF16), wu1=ffn1_w_up[l].astype(_BF16),
            wd1=ffn1_w_down[l].astype(_BF16), gm=row(mix_norm), win=w_in[l].astype(_BF16),
            qg=jnp.tile(row(q_norm), (1, N_HEADS)) * (HEAD_DIM ** -0.5),
            kg=jnp.tile(row(k_norm), (1, N_HEADS)),
            bias=_bias_table(rel_pos_bias[l].astype(_F32)), cw=conv_w[l].astype(_F32),
            ag=row(attn_out_norm), cg=row(conv_out_norm), wo=w_out[l].astype(_BF16),
            g2=row(ffn2_norm), wg2=ffn2_w_gate[l].astype(_BF16), wu2=ffn2_w_up[l].astype(_BF16),
            wd2=ffn2_w_down[l].astype(_BF16), gf=row(final_norm),
        )
        y_prompt = _layer(y_prompt, p)
        y_sample = _layer(y_sample, p)
    return (y_prompt, y_sample)
```

```python
import functools

import numpy as np
import jax
import jax.numpy as jnp
from jax import lax
from jax.experimental import pallas as pl
from jax.experimental.pallas import tpu as pltpu

D_MODEL = 1024
GRID_W = 64
N_HEADS = 8
HEAD_DIM = 64
D_ATTN = N_HEADS * HEAD_DIM
D_CONV = D_MODEL - D_ATTN
N_CONV_GROUPS = 8
GROUP_DIM = D_CONV // N_CONV_GROUPS
NA_ROWS = 8
NA_COLS = 16
D_FF = 2816
EPS = 1e-6

LANES = 128
BF16_ROWS = 16
VMEM_LIMIT_BYTES = 56 * 1024 * 1024
TOKEN_TILE = 512
ROWS_PER_TILE = 8
SCORE_LOOKAHEAD = 2
FF_CHUNK = 1024
MASK_VALUE = -1e30
LOG2_E = 1.4426950408889634
Z_COLS = 5 * D_ATTN

_BF16 = jnp.bfloat16
_F32 = jnp.float32


def _const_spec(shape):
    nd = len(shape)
    return pl.BlockSpec(shape, lambda *_: (0,) * nd, pipeline_mode=pl.Buffered(1))


def _rms_rows(x, gain):
    ms = jnp.mean(x * x, axis=-1, keepdims=True)
    return (x * lax.rsqrt(ms + EPS)) * gain


def _group64_norm(x, gain):
    rows, cols = x.shape
    lo = lax.broadcasted_iota(jnp.int32, (rows, LANES), 1) < HEAD_DIM
    outs = []
    for j in range(cols // LANES):
        t = x[:, j * LANES:(j + 1) * LANES]
        sq = t * t
        s_lo = jnp.sum(jnp.where(lo, sq, 0.0), axis=-1, keepdims=True)
        s_hi = jnp.sum(jnp.where(lo, 0.0, sq), axis=-1, keepdims=True)
        r = jnp.where(lo, lax.rsqrt(s_lo * (1.0 / HEAD_DIM) + EPS), lax.rsqrt(s_hi * (1.0 / HEAD_DIM) + EPS))
        outs.append((t * r) * gain[:, j * LANES:(j + 1) * LANES])
    return outs


def _swiglu_residual(x, norm_gain, wg_ref, wu_ref, wd_ref):
    xn = _rms_rows(x, norm_gain).astype(_BF16)
    acc = None
    for c0 in range(0, D_FF, FF_CHUNK):
        c1 = min(c0 + FF_CHUNK, D_FF)
        g = jnp.dot(xn, wg_ref[:, c0:c1], preferred_element_type=_F32)
        u = jnp.dot(xn, wu_ref[:, c0:c1], preferred_element_type=_F32)
        a = ((g / (1.0 + jnp.exp(-g))) * u).astype(_BF16)
        d = jnp.dot(a, wd_ref[c0:c1, :], preferred_element_type=_F32)
        acc = d if acc is None else acc + d
    return x + 0.5 * acc


def _ffn_inproj_kernel(x_ref, g1_ref, wg_ref, wu_ref, wd_ref, g2_ref, win_ref, qg_ref, kg_ref, h_ref, z_ref):
    h = _swiglu_residual(x_ref[...], g1_ref[...], wg_ref, wu_ref, wd_ref)
    h_ref[...] = h
    un = _rms_rows(h, g2_ref[...]).astype(_BF16)

    def proj(j):
        return jnp.dot(un, win_ref[:, j * D_ATTN:(j + 1) * D_ATTN], preferred_element_type=_F32)

    for j, gain_ref in ((0, qg_ref), (1, kg_ref)):
        parts = _group64_norm(proj(j), gain_ref[...])
        for t, part in enumerate(parts):
            z_ref[:, j * D_ATTN + t * LANES: j * D_ATTN + (t + 1) * LANES] = part.astype(_BF16)
    z_ref[:, 2 * D_ATTN:3 * D_ATTN] = proj(2).astype(_BF16)
    z_ref[:, 3 * D_ATTN:4 * D_ATTN] = proj(3).astype(_BF16)
    z_ref[:, 4 * D_ATTN:5 * D_ATTN] = (proj(4) * proj(5)).astype(_BF16)


def _ffn_inproj(x2d, g1, wg, wu, wd, g2, win, qg, kg):
    n = x2d.shape[0]
    tm = TOKEN_TILE
    assert n % tm == 0
    return pl.pallas_call(
        _ffn_inproj_kernel,
        grid=(n // tm,),
        in_specs=[
            pl.BlockSpec((tm, D_MODEL), lambda i: (i, 0)),
            _const_spec((1, D_MODEL)),
            _const_spec((D_MODEL, D_FF)),
            _const_spec((D_MODEL, D_FF)),
            _const_spec((D_FF, D_MODEL)),
            _const_spec((1, D_MODEL)),
            _const_spec((D_MODEL, 6 * D_ATTN)),
            _const_spec((1, D_ATTN)),
            _const_spec((1, D_ATTN)),
        ],
        out_specs=[
            pl.BlockSpec((tm, D_MODEL), lambda i: (i, 0)),
            pl.BlockSpec((tm, Z_COLS), lambda i: (i, 0)),
        ],
        out_shape=[
            jax.ShapeDtypeStruct((n, D_MODEL), _F32),
            jax.ShapeDtypeStruct((n, Z_COLS), _BF16),
        ],
        compiler_params=pltpu.CompilerParams(
            dimension_semantics=("parallel",), vmem_limit_bytes=VMEM_LIMIT_BYTES),
        name="ffn1_inproj",
    )(x2d, g1, wg, wu, wd, g2, win, qg, kg)


def _mixer_kernel(q_ref, k_ref, v_ref, gb_ref, u_ref, h_ref, bias_ref, cw_ref, ag_ref, cg_ref, wo_ref,
                  o_ref, attn_scr, *, rows, seq):
    i = pl.program_id(1)
    n_tiles = pl.num_programs(1)
    tq = ROWS_PER_TILE * GRID_W
    n_keys = NA_ROWS * GRID_W
    lo_q = lax.broadcasted_iota(jnp.int32, (GRID_W, LANES), 1) < HEAD_DIM

    pairs = [slice(j * LANES, (j + 1) * LANES) for j in range(D_ATTN // LANES)]

    def key_start(rr):
        r = i * ROWS_PER_TILE + rr
        rs = jnp.clip(r - NA_ROWS // 2, 0, rows - NA_ROWS)
        return rs - r, pl.multiple_of(rs * GRID_W, GRID_W)

    def scores(rr):
        _, key0 = key_start(rr)
        out = []
        for lanes in pairs:
            kp = k_ref[pl.ds(key0, n_keys), lanes]
            qp = q_ref[rr * GRID_W:(rr + 1) * GRID_W, lanes]
            zero = jnp.zeros_like(qp)
            q2 = jnp.concatenate([jnp.where(lo_q, qp, zero), jnp.where(lo_q, zero, qp)], axis=0)
            out.append(lax.dot_general(q2, kp, (((1,), (1,)), ((), ())), preferred_element_type=_F32))
        return out

    def softmax(rr, s_list):
        row_off, _ = key_start(rr)
        out = []
        for lanes, s in zip(pairs, s_list):
            s = s + jnp.concatenate(
                [bias_ref[2 * t + row_off + (NA_ROWS - 1), lanes, :] for t in range(NA_ROWS // 2)], axis=-1)
            p = jnp.exp2(s - jnp.max(s, axis=-1, keepdims=True))
            out.append(p.astype(_BF16))
        return out

    ones_cols = jnp.ones((n_keys, LANES), _BF16)

    def weighted_values(rr, p_list):
        _, key0 = key_start(rr)
        for lanes, p in zip(pairs, p_list):
            vp = jnp.concatenate([v_ref[pl.ds(key0, n_keys), lanes], ones_cols], axis=-1)
            ol = jnp.dot(p, vp, preferred_element_type=_F32)
            o2 = ol[:, :LANES] / ol[:, LANES:]
            o = jnp.where(lo_q, o2[:GRID_W], o2[GRID_W:])
            attn_scr[rr * GRID_W:(rr + 1) * GRID_W, lanes] = _group64_norm(o, ag_ref[:, lanes])[0].astype(_BF16)

    def conv_branch():
        t0 = pl.multiple_of(i * tq, tq)
        u_c = u_ref[pl.ds(t0, tq), :].astype(_F32)
        prev_blk = u_ref[pl.ds(pl.multiple_of(jnp.maximum(t0 - BF16_ROWS, 0), BF16_ROWS), BF16_ROWS), :]
        next_blk = u_ref[pl.ds(pl.multiple_of(jnp.minimum(t0 + tq, seq - BF16_ROWS), BF16_ROWS), BF16_ROWS), :]
        prev_row = prev_blk.astype(_F32)[BF16_ROWS - 1:BF16_ROWS, :] * (i > 0).astype(_F32)
        next_row = next_blk.astype(_F32)[0:1, :] * (i < n_tiles - 1).astype(_F32)
        tok = lax.broadcasted_iota(jnp.int32, (tq, D_CONV), 0)
        u_prev = jnp.where(tok == 0, prev_row, pltpu.roll(u_c, 1, 0))
        u_next = jnp.where(tok == tq - 1, next_row, pltpu.roll(u_c, tq - 1, 0))
        cw = cw_ref[...]
        y = u_prev * cw[0:1, :] + u_c * cw[1:2, :] + u_next * cw[2:3, :]
        c = gb_ref[...].astype(_F32) * y
        c_n = jnp.concatenate([t.astype(_BF16) for t in _group64_norm(c, cg_ref[...])], axis=-1)
        return h_ref[...] + jnp.dot(c_n, wo_ref[D_ATTN:, :], preferred_element_type=_F32)

    pending = [scores(rr) for rr in range(SCORE_LOOKAHEAD)]
    for rr in range(ROWS_PER_TILE):
        if rr + SCORE_LOOKAHEAD < ROWS_PER_TILE:
            pending.append(scores(rr + SCORE_LOOKAHEAD))
        weighted_values(rr, softmax(rr, pending.pop(0)))
        if rr == ROWS_PER_TILE // 2 - 1:
            o_ref[...] = conv_branch()

    o_ref[...] += jnp.dot(attn_scr[...], wo_ref[:D_ATTN, :], preferred_element_type=_F32)


def _mixer(z3d, h3d, bias_tab, conv_w, attn_gain, conv_gain, w_out):
    b, s, _ = z3d.shape
    rows = s // GRID_W
    tq = ROWS_PER_TILE * GRID_W
    assert s % tq == 0 and rows >= NA_ROWS

    def tile(width, col):
        return pl.BlockSpec((None, tq, width), lambda bi, ti: (bi, ti, col))

    def whole_seq(col):
        return pl.BlockSpec((None, s, D_ATTN), lambda bi, ti: (bi, 0, col))

    return pl.pallas_call(
        functools.partial(_mixer_kernel, rows=rows, seq=s),
        grid=(b, s // tq),
        in_specs=[
            tile(D_ATTN, 0),
            whole_seq(1),
            whole_seq(2),
            tile(D_ATTN, 3),
            whole_seq(4),
            tile(D_MODEL, 0),
            _const_spec(bias_tab.shape),
            _const_spec(conv_w.shape),
            _const_spec((1, D_ATTN)),
            _const_spec((1, D_CONV)),
            _const_spec((D_MODEL, D_MODEL)),
        ],
        out_specs=pl.BlockSpec((None, tq, D_MODEL), lambda bi, ti: (bi, ti, 0)),
        out_shape=jax.ShapeDtypeStruct((b, s, D_MODEL), _F32),
        scratch_shapes=[pltpu.VMEM((tq, D_ATTN), _BF16)],
        compiler_params=pltpu.CompilerParams(
            dimension_semantics=("parallel", "parallel"), vmem_limit_bytes=VMEM_LIMIT_BYTES),
        name="mixer",
    )(z3d, z3d, z3d, z3d, z3d, h3d, bias_tab, conv_w, attn_gain, conv_gain, w_out)


def _ffn_out_kernel(x_ref, g1_ref, wg_ref, wu_ref, wd_ref, gf_ref, y_ref):
    h = _swiglu_residual(x_ref[...], g1_ref[...], wg_ref, wu_ref, wd_ref)
    y_ref[...] = _rms_rows(h, gf_ref[...])


def _ffn_out(x2d, g1, wg, wu, wd, gf):
    n = x2d.shape[0]
    tm = TOKEN_TILE
    assert n % tm == 0
    return pl.pallas_call(
        _ffn_out_kernel,
        grid=(n // tm,),
        in_specs=[
            pl.BlockSpec((tm, D_MODEL), lambda i: (i, 0)),
            _const_spec((1, D_MODEL)),
            _const_spec((D_MODEL, D_FF)),
            _const_spec((D_MODEL, D_FF)),
            _const_spec((D_FF, D_MODEL)),
            _const_spec((1, D_MODEL)),
        ],
        out_specs=pl.BlockSpec((tm, D_MODEL), lambda i: (i, 0)),
        out_shape=jax.ShapeDtypeStruct((n, D_MODEL), _F32),
        compiler_params=pltpu.CompilerParams(
            dimension_semantics=("parallel",), vmem_limit_bytes=VMEM_LIMIT_BYTES),
        name="ffn2_final",
    )(x2d, g1, wg, wu, wd, gf)


def _bias_table(rpb):
    qc = np.arange(GRID_W)[:, None]
    kc = np.arange(GRID_W)[None, :]
    start = np.clip(qc - NA_COLS // 2, 0, GRID_W - NA_COLS)
    valid = (kc >= start) & (kc < start + NA_COLS)
    onehot = (np.arange(2 * NA_COLS - 1)[:, None, None] == (kc - qc + NA_COLS - 1)[None]) & valid[None]
    cb = jnp.einsum("hrj,jck->rhck", rpb, jnp.asarray(onehot, _F32), precision=lax.Precision.HIGHEST)
    cb = jnp.where(valid[None, None], cb * LOG2_E, MASK_VALUE)
    pair = jnp.concatenate([cb[:-1], cb[1:]], axis=-1)
    return pair.reshape(2 * NA_ROWS - 2, N_HEADS * GRID_W, 2 * GRID_W).astype(_F32)


def _layer(x, p):
    b, s, _ = x.shape
    h, z = _ffn_inproj(x.reshape(b * s, D_MODEL), p["g1"], p["wg1"], p["wu1"], p["wd1"], p["gm"], p["win"],
                       p["qg"], p["kg"])
    h2 = _mixer(z.reshape(b, s, Z_COLS), h.reshape(b, s, D_MODEL), p["bias"], p["cw"], p["ag"], p["cg"], p["wo"])
    y = _ffn_out(h2.reshape(b * s, D_MODEL), p["g2"], p["wg2"], p["wu2"], p["wd2"], p["gf"])
    return y.reshape(b, s, D_MODEL)


def kernel(x_prompt, x_sample, ffn1_norm, ffn1_w_gate, ffn1_w_up, ffn1_w_down, mix_norm, w_in, q_norm, k_norm, rel_pos_bias, conv_w, attn_out_norm, conv_out_norm, w_out, ffn2_norm, ffn2_w_gate, ffn2_w_up, ffn2_w_down, final_norm):
    depth = ffn1_norm.shape[0]
    y_prompt, y_sample = x_prompt, x_sample
    for l in range(depth):
        row = lambda a: a[l].reshape(1, -1).astype(_F32)
        p = dict(
            g1=row(ffn1_norm), wg1=ffn1_w_gate[l].astype(_BF16), wu1=ffn1_w_up[l].astype(_BF16),
            wd1=ffn1_w_down[l].astype(_BF16), gm=row(mix_norm), win=w_in[l].astype(_BF16),
            qg=jnp.tile(row(q_norm), (1, N_HEADS)) * (HEAD_DIM ** -0.5 * LOG2_E),
            kg=jnp.tile(row(k_norm), (1, N_HEADS)),
            bias=_bias_table(rel_pos_bias[l].astype(_F32)), cw=conv_w[l].astype(_F32),
            ag=row(attn_out_norm), cg=row(conv_out_norm), wo=w_out[l].astype(_BF16),
            g2=row(ffn2_norm), wg2=ffn2_w_gate[l].astype(_BF16), wu2=ffn2_w_up[l].astype(_BF16),
            wd2=ffn2_w_down[l].astype(_BF16), gf=row(final_norm),
        )
        y_prompt = _layer(y_prompt, p)
        y_sample = _layer(y_sample, p)
    return (y_prompt, y_sample)
```

```python
import functools

import numpy as np
import jax
import jax.numpy as jnp
from jax import lax
from jax.experimental import pallas as pl
from jax.experimental.pallas import tpu as pltpu

D_MODEL = 1024
GRID_W = 64
N_HEADS = 8
HEAD_DIM = 64
D_ATTN = N_HEADS * HEAD_DIM
D_CONV = D_MODEL - D_ATTN
N_CONV_GROUPS = 8
GROUP_DIM = D_CONV // N_CONV_GROUPS
NA_ROWS = 8
NA_COLS = 16
D_FF = 2816
EPS = 1e-6

LANES = 128
BF16_ROWS = 16
VMEM_LIMIT_BYTES = 56 * 1024 * 1024
TOKEN_TILE = 512
FFN_OUT_TILE = 2 * TOKEN_TILE
ROWS_PER_TILE = 8
FF_CHUNK = 1024
MASK_VALUE = -1e30
LOG2_E = 1.4426950408889634
Z_COLS = 5 * D_ATTN

_BF16 = jnp.bfloat16
_F32 = jnp.float32


def _const_spec(shape):
    nd = len(shape)
    return pl.BlockSpec(shape, lambda *_: (0,) * nd, pipeline_mode=pl.Buffered(1))


def _rms_rows(x, gain):
    ms = jnp.mean(x * x, axis=-1, keepdims=True)
    return (x * lax.rsqrt(ms + EPS)) * gain


def _group64_norm(x, gain):
    rows, cols = x.shape
    lo = lax.broadcasted_iota(jnp.int32, (rows, LANES), 1) < HEAD_DIM
    outs = []
    for j in range(cols // LANES):
        t = x[:, j * LANES:(j + 1) * LANES]
        sq = t * t
        s_lo = jnp.sum(jnp.where(lo, sq, 0.0), axis=-1, keepdims=True)
        s_hi = jnp.sum(jnp.where(lo, 0.0, sq), axis=-1, keepdims=True)
        r = jnp.where(lo, lax.rsqrt(s_lo * (1.0 / HEAD_DIM) + EPS), lax.rsqrt(s_hi * (1.0 / HEAD_DIM) + EPS))
        outs.append((t * r) * gain[:, j * LANES:(j + 1) * LANES])
    return outs


def _swiglu_residual(x, norm_gain, wg_ref, wu_ref, wd_ref):
    xn = _rms_rows(x, norm_gain).astype(_BF16)
    acc = None
    for c0 in range(0, D_FF, FF_CHUNK):
        c1 = min(c0 + FF_CHUNK, D_FF)
        g = jnp.dot(xn, wg_ref[:, c0:c1], preferred_element_type=_F32)
        u = jnp.dot(xn, wu_ref[:, c0:c1], preferred_element_type=_F32)
        a = ((g / (1.0 + jnp.exp(-g))) * u).astype(_BF16)
        d = jnp.dot(a, wd_ref[c0:c1, :], preferred_element_type=_F32)
        acc = d if acc is None else acc + d
    return x + 0.5 * acc


def _ffn_inproj_kernel(x_ref, g1_ref, wg_ref, wu_ref, wd_ref, g2_ref, win_ref, qg_ref, kg_ref, h_ref, z_ref):
    h = _swiglu_residual(x_ref[...], g1_ref[...], wg_ref, wu_ref, wd_ref)
    h_ref[...] = h
    un = _rms_rows(h, g2_ref[...]).astype(_BF16)

    def proj(j):
        return jnp.dot(un, win_ref[:, j * D_ATTN:(j + 1) * D_ATTN], preferred_element_type=_F32)

    for j, gain_ref in ((0, qg_ref), (1, kg_ref)):
        parts = _group64_norm(proj(j), gain_ref[...])
        for t, part in enumerate(parts):
            z_ref[:, j * D_ATTN + t * LANES: j * D_ATTN + (t + 1) * LANES] = part.astype(_BF16)
    z_ref[:, 2 * D_ATTN:3 * D_ATTN] = proj(2).astype(_BF16)
    z_ref[:, 3 * D_ATTN:4 * D_ATTN] = proj(3).astype(_BF16)
    z_ref[:, 4 * D_ATTN:5 * D_ATTN] = (proj(4) * proj(5)).astype(_BF16)


def _ffn_inproj(x2d, g1, wg, wu, wd, g2, win, qg, kg):
    n = x2d.shape[0]
    tm = TOKEN_TILE
    assert n % tm == 0
    return pl.pallas_call(
        _ffn_inproj_kernel,
        grid=(n // tm,),
        in_specs=[
            pl.BlockSpec((tm, D_MODEL), lambda i: (i, 0)),
            _const_spec((1, D_MODEL)),
            _const_spec((D_MODEL, D_FF)),
            _const_spec((D_MODEL, D_FF)),
            _const_spec((D_FF, D_MODEL)),
            _const_spec((1, D_MODEL)),
            _const_spec((D_MODEL, 6 * D_ATTN)),
            _const_spec((1, D_ATTN)),
            _const_spec((1, D_ATTN)),
        ],
        out_specs=[
            pl.BlockSpec((tm, D_MODEL), lambda i: (i, 0)),
            pl.BlockSpec((tm, Z_COLS), lambda i: (i, 0)),
        ],
        out_shape=[
            jax.ShapeDtypeStruct((n, D_MODEL), _F32),
            jax.ShapeDtypeStruct((n, Z_COLS), _BF16),
        ],
        compiler_params=pltpu.CompilerParams(
            dimension_semantics=("parallel",), vmem_limit_bytes=VMEM_LIMIT_BYTES),
        name="ffn1_inproj",
    )(x2d, g1, wg, wu, wd, g2, win, qg, kg)


def _mixer_kernel(q_ref, k_ref, v_ref, gb_ref, u_ref, h_ref, bias_ref, cw_ref, ag_ref, cg_ref, wo_ref,
                  o_ref, attn_scr, *, rows, seq):
    i = pl.program_id(1)
    n_tiles = pl.num_programs(1)
    tq = ROWS_PER_TILE * GRID_W
    n_keys = NA_ROWS * GRID_W
    lo_q = lax.broadcasted_iota(jnp.int32, (GRID_W, LANES), 1) < HEAD_DIM
    ones_cols = jnp.ones((n_keys, LANES), _BF16)

    t0 = pl.multiple_of(i * tq, tq)
    u_c = u_ref[pl.ds(t0, tq), :].astype(_F32)
    prev_blk = u_ref[pl.ds(pl.multiple_of(jnp.maximum(t0 - BF16_ROWS, 0), BF16_ROWS), BF16_ROWS), :]
    next_blk = u_ref[pl.ds(pl.multiple_of(jnp.minimum(t0 + tq, seq - BF16_ROWS), BF16_ROWS), BF16_ROWS), :]
    prev_row = prev_blk.astype(_F32)[BF16_ROWS - 1:BF16_ROWS, :] * (i > 0).astype(_F32)
    next_row = next_blk.astype(_F32)[0:1, :] * (i < n_tiles - 1).astype(_F32)
    tok = lax.broadcasted_iota(jnp.int32, (tq, D_CONV), 0)
    u_prev = jnp.where(tok == 0, prev_row, pltpu.roll(u_c, 1, 0))
    u_next = jnp.where(tok == tq - 1, next_row, pltpu.roll(u_c, tq - 1, 0))
    cw = cw_ref[...]
    y = u_prev * cw[0:1, :] + u_c * cw[1:2, :] + u_next * cw[2:3, :]
    c = gb_ref[...].astype(_F32) * y
    c_n = jnp.concatenate([t.astype(_BF16) for t in _group64_norm(c, cg_ref[...])], axis=-1)

    for rr in range(ROWS_PER_TILE):
        r = i * ROWS_PER_TILE + rr
        rs = jnp.clip(r - NA_ROWS // 2, 0, rows - NA_ROWS)
        key0 = pl.multiple_of(rs * GRID_W, GRID_W)
        for j in range(D_ATTN // LANES):
            lanes = slice(j * LANES, (j + 1) * LANES)
            kp = k_ref[pl.ds(key0, n_keys), lanes]
            qp = q_ref[rr * GRID_W:(rr + 1) * GRID_W, lanes]
            zero = jnp.zeros_like(qp)
            q2 = jnp.concatenate([jnp.where(lo_q, qp, zero), jnp.where(lo_q, zero, qp)], axis=0)
            s = lax.dot_general(q2, kp, (((1,), (1,)), ((), ())), preferred_element_type=_F32)
            s = s + jnp.concatenate(
                [bias_ref[2 * t + rs - r + (NA_ROWS - 1), lanes, :] for t in range(NA_ROWS // 2)], axis=-1)
            p = jnp.exp2(s - jnp.max(s, axis=-1, keepdims=True))
            vp = jnp.concatenate([v_ref[pl.ds(key0, n_keys), lanes], ones_cols], axis=-1)
            ol = jnp.dot(p.astype(_BF16), vp, preferred_element_type=_F32)
            o2 = ol[:, :LANES] / ol[:, LANES:]
            o = jnp.where(lo_q, o2[:GRID_W], o2[GRID_W:])
            attn_scr[rr * GRID_W:(rr + 1) * GRID_W, lanes] = _group64_norm(o, ag_ref[:, lanes])[0].astype(_BF16)

    o_ref[...] = (h_ref[...] + jnp.dot(attn_scr[...], wo_ref[:D_ATTN, :], preferred_element_type=_F32)
                  + jnp.dot(c_n, wo_ref[D_ATTN:, :], preferred_element_type=_F32))


def _mixer(z3d, h3d, bias_tab, conv_w, attn_gain, conv_gain, w_out):
    b, s, _ = z3d.shape
    rows = s // GRID_W
    tq = ROWS_PER_TILE * GRID_W
    assert s % tq == 0 and rows >= NA_ROWS

    def tile(width, col):
        return pl.BlockSpec((None, tq, width), lambda bi, ti: (bi, ti, col))

    def whole_seq(col):
        return pl.BlockSpec((None, s, D_ATTN), lambda bi, ti: (bi, 0, col))

    return pl.pallas_call(
        functools.partial(_mixer_kernel, rows=rows, seq=s),
        grid=(b, s // tq),
        in_specs=[
            tile(D_ATTN, 0),
            whole_seq(1),
            whole_seq(2),
            tile(D_ATTN, 3),
            whole_seq(4),
            tile(D_MODEL, 0),
            _const_spec(bias_tab.shape),
            _const_spec(conv_w.shape),
            _const_spec((1, D_ATTN)),
            _const_spec((1, D_CONV)),
            _const_spec((D_MODEL, D_MODEL)),
        ],
        out_specs=pl.BlockSpec((None, tq, D_MODEL), lambda bi, ti: (bi, ti, 0)),
        out_shape=jax.ShapeDtypeStruct((b, s, D_MODEL), _F32),
        scratch_shapes=[pltpu.VMEM((tq, D_ATTN), _BF16)],
        compiler_params=pltpu.CompilerParams(
            dimension_semantics=("parallel", "parallel"), vmem_limit_bytes=VMEM_LIMIT_BYTES),
        name="mixer",
    )(z3d, z3d, z3d, z3d, z3d, h3d, bias_tab, conv_w, attn_gain, conv_gain, w_out)


def _ffn_out_kernel(x_ref, g1_ref, wg_ref, wu_ref, wd_ref, gf_ref, y_ref):
    for s in range(FFN_OUT_TILE // TOKEN_TILE):
        rows = slice(s * TOKEN_TILE, (s + 1) * TOKEN_TILE)
        h = _swiglu_residual(x_ref[rows, :], g1_ref[...], wg_ref, wu_ref, wd_ref)
        y_ref[rows, :] = _rms_rows(h, gf_ref[...])


def _ffn_out(x2d, g1, wg, wu, wd, gf):
    n = x2d.shape[0]
    tm = FFN_OUT_TILE
    assert n % tm == 0
    return pl.pallas_call(
        _ffn_out_kernel,
        grid=(n // tm,),
        in_specs=[
            pl.BlockSpec((tm, D_MODEL), lambda i: (i, 0)),
            _const_spec((1, D_MODEL)),
            _const_spec((D_MODEL, D_FF)),
            _const_spec((D_MODEL, D_FF)),
            _const_spec((D_FF, D_MODEL)),
            _const_spec((1, D_MODEL)),
        ],
        out_specs=pl.BlockSpec((tm, D_MODEL), lambda i: (i, 0)),
        out_shape=jax.ShapeDtypeStruct((n, D_MODEL), _F32),
        compiler_params=pltpu.CompilerParams(
            dimension_semantics=("parallel",), vmem_limit_bytes=VMEM_LIMIT_BYTES),
        name="ffn2_final",
    )(x2d, g1, wg, wu, wd, gf)


def _bias_table(rpb):
    qc = np.arange(GRID_W)[:, None]
    kc = np.arange(GRID_W)[None, :]
    start = np.clip(qc - NA_COLS // 2, 0, GRID_W - NA_COLS)
    valid = (kc >= start) & (kc < start + NA_COLS)
    onehot = (np.arange(2 * NA_COLS - 1)[:, None, None] == (kc - qc + NA_COLS - 1)[None]) & valid[None]
    cb = jnp.einsum("hrj,jck->rhck", rpb, jnp.asarray(onehot, _F32), precision=lax.Precision.HIGHEST)
    cb = jnp.where(valid[None, None], cb * LOG2_E, MASK_VALUE)
    pair = jnp.concatenate([cb[:-1], cb[1:]], axis=-1)
    return pair.reshape(2 * NA_ROWS - 2, N_HEADS * GRID_W, 2 * GRID_W).astype(_F32)


def _layer(x, p):
    b, s, _ = x.shape
    h, z = _ffn_inproj(x.reshape(b * s, D_MODEL), p["g1"], p["wg1"], p["wu1"], p["wd1"], p["gm"], p["win"],
                       p["qg"], p["kg"])
    h2 = _mixer(z.reshape(b, s, Z_COLS), h.reshape(b, s, D_MODEL), p["bias"], p["cw"], p["ag"], p["cg"], p["wo"])
    y = _ffn_out(h2.reshape(b * s, D_MODEL), p["g2"], p["wg2"], p["wu2"], p["wd2"], p["gf"])
    return y.reshape(b, s, D_MODEL)


def kernel(x_prompt, x_sample, ffn1_norm, ffn1_w_gate, ffn1_w_up, ffn1_w_down, mix_norm, w_in, q_norm, k_norm, rel_pos_bias, conv_w, attn_out_norm, conv_out_norm, w_out, ffn2_norm, ffn2_w_gate, ffn2_w_up, ffn2_w_down, final_norm):
    depth = ffn1_norm.shape[0]
    y_prompt, y_sample = x_prompt, x_sample
    for l in range(depth):
        row = lambda a: a[l].reshape(1, -1).astype(_F32)
        p = dict(
            g1=row(ffn1_norm), wg1=ffn1_w_gate[l].astype(_BF16), wu1=ffn1_w_up[l].astype(_BF16),
            wd1=ffn1_w_down[l].astype(_BF16), gm=row(mix_norm), win=w_in[l].astype(_BF16),
            qg=jnp.tile(row(q_norm), (1, N_HEADS)) * (HEAD_DIM ** -0.5 * LOG2_E),
            kg=jnp.tile(row(k_norm), (1, N_HEADS)),
            bias=_bias_table(rel_pos_bias[l].astype(_F32)), cw=conv_w[l].astype(_F32),
            ag=row(attn_out_norm), cg=row(conv_out_norm), wo=w_out[l].astype(_BF16),
            g2=row(ffn2_norm), wg2=ffn2_w_gate[l].astype(_BF16), wu2=ffn2_w_up[l].astype(_BF16),
            wd2=ffn2_w_down[l].astype(_BF16), gf=row(final_norm),
        )
        y_prompt = _layer(y_prompt, p)
        y_sample = _layer(y_sample, p)
    return (y_prompt, y_sample)
```
